```python
import math
import jax, jax.numpy as jnp
from jax import lax
import numpy as np

D_MODEL = 4096
BATCH = 4
SEQ = 2048
DEPTH = 4
DEC_BATCH = 128
DEC_SEQ = 4
PAST_LEN = 16384
PAGE_SIZE = 128

MIX_WIDTH = D_MODEL
POOL_WIDTH = MIX_WIDTH // 4
POOL_WINDOWS = (2, 4, 8, 16)
POOL_GROUPS = len(POOL_WINDOWS)
POOL_GROUP_DIM = POOL_WIDTH // POOL_GROUPS
POOL_BUF = max(POOL_WINDOWS) - 1
DN_WIDTH = MIX_WIDTH - POOL_WIDTH
DN_HEAD_DIM = 128
DN_HEADS = DN_WIDTH // DN_HEAD_DIM
CONV_WIDTH = 4
CHUNK = 64
IN_COLS = POOL_WIDTH + 4 * DN_WIDTH + 2 * DN_HEADS
PEER_HEADS = 8
N_KEYS = 128
N_EXPERTS = N_KEYS * N_KEYS
PEER_KEY_DIM = 256
PEER_HALF = PEER_KEY_DIM // 2
PEER_TOPK = 16
PEER_BLOCK = 64
N_MOD = 6
EPS = 1e-6

kernel_name = "pool_deltanet_peer_hybrid_step"


def rmsnorm(x, w):
    xf = x.astype(jnp.float32)
    y = xf * lax.rsqrt(jnp.mean(xf * xf, axis=-1, keepdims=True) + EPS)
    return (y * w.astype(jnp.float32)).astype(x.dtype)


def l2norm(x):
    return x * lax.rsqrt(jnp.sum(x * x, axis=-1, keepdims=True) + EPS)


def short_conv(x, buf, w):
    L = x.shape[1]
    xp = jnp.concatenate([buf.astype(x.dtype), x], axis=1)
    y = xp[:, 0:L] * w[0]
    for j in range(1, CONV_WIDTH):
        y = y + xp[:, j:j + L] * w[j]
    return jax.nn.silu(y), xp[:, -(CONV_WIDTH - 1):]


def pool_mixer(p, buf, start, w_pool, scale):
    B, L, C = p.shape
    xp = jnp.concatenate([buf.astype(p.dtype), p], axis=1)
    xf = xp.astype(jnp.float32)
    cs0 = jnp.concatenate([jnp.zeros((B, 1, C), jnp.float32), jnp.cumsum(xf, axis=1)], axis=1)
    end = cs0[:, POOL_BUF + 1:]
    pos = start + jnp.arange(L)
    outs = []
    for gi, win in enumerate(POOL_WINDOWS):
        lo, hi = gi * POOL_GROUP_DIM, (gi + 1) * POOL_GROUP_DIM
        begin = cs0[:, POOL_BUF + 1 - win:POOL_BUF + 1 - win + L, lo:hi]
        cnt = jnp.minimum(win, pos + 1).astype(jnp.float32)[None, :, None]
        mean = (end[..., lo:hi] - begin) / cnt
        outs.append(mean - xf[:, POOL_BUF:, lo:hi])
    pooled = jnp.stack(outs, axis=2)
    y = jnp.einsum('blgc,gcd->blgd', pooled, w_pool.astype(jnp.float32)).reshape(B, L, C) * scale.astype(jnp.float32)
    return y.astype(p.dtype), xp[:, -POOL_BUF:]


def gated_delta_rule(q, k, v, g, beta, s0):
    B, L, H, DK = q.shape
    DV = v.shape[-1]
    C = min(CHUNK, L)
    n = -(-L // C)
    pad = n * C - L

    def prep(t):
        t = jnp.pad(t, [(0, 0), (0, pad)] + [(0, 0)] * (t.ndim - 2))
        t = t.reshape((B, n, C) + t.shape[2:])
        return jnp.transpose(t, (1, 0, 3, 2) + tuple(range(4, t.ndim)))

    q, k, v, g, beta = prep(q), prep(k), prep(v), prep(g), prep(beta)
    G = jnp.cumsum(g, axis=-1)
    tri = jnp.tril(jnp.ones((C, C), bool))
    strict = jnp.tril(jnp.ones((C, C), bool), -1)
    diff = G[..., :, None] - G[..., None, :]
    decay = jnp.where(tri, jnp.exp(jnp.where(tri, diff, 0.0)), 0.0)
    kb = k * beta[..., None]
    kk = jnp.einsum('nbhcd,nbhsd->nbhcs', kb, k) * decay
    a_mat = jnp.where(strict, kk, 0.0) + jnp.eye(C, dtype=kk.dtype)
    rhs = jnp.concatenate([v * beta[..., None], kb * jnp.exp(G)[..., None]], axis=-1)
    sol = lax.linalg.triangular_solve(a_mat, rhs, left_side=True, lower=True, unit_diagonal=True)
    u, w = sol[..., :DV], sol[..., DV:]
    attn = jnp.einsum('nbhcd,nbhsd->nbhcs', q, k) * decay
    q_dec = q * jnp.exp(G)[..., None]
    g_last = G[..., -1]
    k_dec = k * jnp.exp(g_last[..., None] - G)[..., None]

    def step(S, xs):
        w_c, u_c, q_c, a_c, k_c, gl = xs
        v_new = u_c - jnp.einsum('bhcd,bhde->bhce', w_c, S)
        o_c = jnp.einsum('bhcd,bhde->bhce', q_c, S) + jnp.einsum('bhcs,bhse->bhce', a_c, v_new)
        S = S * jnp.exp(gl)[..., None, None] + jnp.einsum('bhcd,bhce->bhde', k_c, v_new)
        return S, o_c

    S, o = lax.scan(step, s0, (w, u, q_dec, attn, k_dec, g_last))
    o = jnp.transpose(o, (1, 0, 3, 2, 4)).reshape(B, n * C, H, DV)[:, :L]
    return o, S


def peer_ffn(h, w_q, sub_keys, u_tab, v_tab):
    B, L, D = h.shape
    N = B * L
    nb = -(-N // PEER_BLOCK)
    ht = jnp.pad(h.reshape(N, D), ((0, nb * PEER_BLOCK - N), (0, 0)))
    q = (ht @ w_q).astype(jnp.float32).reshape(-1, PEER_HEADS, 2, PEER_HALF)
    s = jnp.einsum('nhpd,hpkd->nhpk', q, sub_keys.astype(jnp.float32))
    s1, i1 = lax.top_k(s[:, :, 0], PEER_TOPK)
    s2, i2 = lax.top_k(s[:, :, 1], PEER_TOPK)
    cand = (s1[..., :, None] + s2[..., None, :]).reshape(-1, PEER_HEADS, PEER_TOPK * PEER_TOPK)
    cidx = (i1[..., :, None] * N_KEYS + i2[..., None, :]).reshape(-1, PEER_HEADS, PEER_TOPK * PEER_TOPK)
    top_s, top_i = lax.top_k(cand, PEER_TOPK)
    eidx = jnp.take_along_axis(cidx, top_i, axis=-1)
    gate = jax.nn.softmax(top_s, axis=-1)

    def block(args):
        hb, eb, gb = args
        u = jnp.take(u_tab, eb, axis=0)
        act = jax.nn.gelu(jnp.einsum('td,thkd->thk', hb, u).astype(jnp.float32), approximate=False)
        vv = jnp.take(v_tab, eb, axis=0)
        return jnp.einsum('thk,thkd->td', (gb * act).astype(h.dtype), vv)

    out = lax.map(block, (ht.reshape(nb, PEER_BLOCK, D),
                          eidx.reshape(nb, PEER_BLOCK, PEER_HEADS, PEER_TOPK),
                          gate.reshape(nb, PEER_BLOCK, PEER_HEADS, PEER_TOPK)))
    return out.reshape(nb * PEER_BLOCK, D)[:N].reshape(B, L, D)


def hybrid_layer(x, c, st_delta, st_conv, st_pool, start,
                 w_ada, b_ada, norm_mix, norm_ffn, w_in, conv_w, a_log, dt_bias, norm_o,
                 pool_w, pool_scale, w_out, peer_wq, peer_keys, peer_u, peer_v):
    B, L, D = x.shape
    f32 = jnp.float32
    mod = jax.nn.silu(c) @ w_ada + b_ada
    sh1, sc1, g1, sh2, sc2, g2 = jnp.split(mod[:, None, :], N_MOD, axis=-1)
    h = rmsnorm(x, norm_mix) * (1 + sc1) + sh1
    proj = h @ w_in
    o1 = POOL_WIDTH
    o2 = o1 + 3 * DN_WIDTH
    o3 = o2 + DN_WIDTH
    o4 = o3 + DN_HEADS
    p, qkv, z, b, a = proj[..., :o1], proj[..., o1:o2], proj[..., o2:o3], proj[..., o3:o4], proj[..., o4:]
    pool_out, new_pool = pool_mixer(p, st_pool, start, pool_w, pool_scale)
    qkv, new_conv = short_conv(qkv, st_conv, conv_w)
    qkv = qkv.astype(f32).reshape(B, L, 3, DN_HEADS, DN_HEAD_DIM)
    q = l2norm(qkv[:, :, 0]) * (DN_HEAD_DIM ** -0.5)
    k = l2norm(qkv[:, :, 1])
    v = qkv[:, :, 2]
    beta = jax.nn.sigmoid(b.astype(f32))
    g = -jnp.exp(a_log.astype(f32)) * jax.nn.softplus(a.astype(f32) + dt_bias.astype(f32))
    o, new_delta = gated_delta_rule(q, k, v, g, beta, st_delta.astype(f32))
    o = rmsnorm(o, norm_o) * jax.nn.silu(z.astype(f32).reshape(B, L, DN_HEADS, DN_HEAD_DIM))
    mix = jnp.concatenate([pool_out, o.reshape(B, L, DN_WIDTH).astype(x.dtype)], axis=-1) @ w_out
    x = x + g1 * mix
    h = rmsnorm(x, norm_ffn) * (1 + sc2) + sh2
    x = x + g2 * peer_ffn(h, peer_wq, peer_keys, peer_u, peer_v)
    return x, new_delta.astype(st_delta.dtype), new_conv, new_pool


def setup_inputs(seed: int = 0) -> dict:
    key = jax.random.key(seed)
    ks = jax.random.split(key, 24)
    f32 = jnp.float32

    def nrm(k, shape, s):
        return jax.random.normal(k, shape, f32) * s

    x_prompt = nrm(ks[0], (BATCH, SEQ, D_MODEL), 1.0)
    x_sample = nrm(ks[1], (DEC_BATCH, DEC_SEQ, D_MODEL), 1.0)
    c_prompt = nrm(ks[2], (BATCH, D_MODEL), 1.0)
    c_sample = nrm(ks[3], (DEC_BATCH, D_MODEL), 1.0)
    state_delta = nrm(ks[4], (DEPTH, DEC_BATCH, DN_HEADS, DN_HEAD_DIM, DN_HEAD_DIM), 0.1)
    state_conv = nrm(ks[5], (DEPTH, DEC_BATCH, CONV_WIDTH - 1, 3 * DN_WIDTH), 1.0)
    state_pool = nrm(ks[6], (DEPTH, DEC_BATCH, POOL_BUF, POOL_WIDTH), 1.0)
    w_ada = nrm(ks[7], (DEPTH, D_MODEL, N_MOD * D_MODEL), 0.5 * D_MODEL ** -0.5)
    b_ada = nrm(ks[8], (DEPTH, N_MOD * D_MODEL), 0.02)
    norm_mix = 1.0 + nrm(ks[9], (DEPTH, D_MODEL), 0.02)
    norm_ffn = 1.0 + nrm(ks[10], (DEPTH, D_MODEL), 0.02)
    w_in = nrm(ks[11], (DEPTH, D_MODEL, IN_COLS), D_MODEL ** -0.5)
    conv_w = nrm(ks[12], (DEPTH, CONV_WIDTH, 3 * DN_WIDTH), CONV_WIDTH ** -0.5)
    a_log = jnp.log(jax.random.uniform(ks[13], (DEPTH, DN_HEADS), f32, 1.0, 16.0))
    dt = jnp.exp(jax.random.uniform(ks[14], (DEPTH, DN_HEADS), f32, math.log(1e-3), math.log(1e-1)))
    dt_bias = dt + jnp.log(-jnp.expm1(-dt))
    norm_o = 1.0 + nrm(ks[15], (DEPTH, DN_HEAD_DIM), 0.02)
    pool_w = nrm(ks[16], (DEPTH, POOL_GROUPS, POOL_GROUP_DIM, POOL_GROUP_DIM), POOL_GROUP_DIM ** -0.5)
    pool_scale = 0.5 + nrm(ks[17], (DEPTH, POOL_WIDTH), 0.05)
    w_out = nrm(ks[18], (DEPTH, MIX_WIDTH, D_MODEL), MIX_WIDTH ** -0.5)
    peer_wq = nrm(ks[19], (DEPTH, D_MODEL, PEER_HEADS * PEER_KEY_DIM), D_MODEL ** -0.5)
    peer_keys = nrm(ks[20], (DEPTH, PEER_HEADS, 2, N_KEYS, PEER_HALF), PEER_HALF ** -0.5)
    peer_u = nrm(ks[21], (DEPTH, N_EXPERTS, D_MODEL), D_MODEL ** -0.5)
    peer_v = nrm(ks[22], (DEPTH, N_EXPERTS, D_MODEL), PEER_HEADS ** -0.5)
    final_norm = 1.0 + nrm(ks[23], (D_MODEL,), 0.02)
    return {"x_prompt": x_prompt, "x_sample": x_sample, "c_prompt": c_prompt, "c_sample": c_sample,
            "state_delta": state_delta, "state_conv": state_conv, "state_pool": state_pool,
            "w_ada": w_ada, "b_ada": b_ada, "norm_mix": norm_mix, "norm_ffn": norm_ffn,
            "w_in": w_in, "conv_w": conv_w, "a_log": a_log, "dt_bias": dt_bias, "norm_o": norm_o,
            "pool_w": pool_w, "pool_scale": pool_scale, "w_out": w_out,
            "peer_wq": peer_wq, "peer_keys": peer_keys, "peer_u": peer_u, "peer_v": peer_v,
            "final_norm": final_norm}


def reference(x_prompt, x_sample, c_prompt, c_sample, state_delta, state_conv, state_pool,
              w_ada, b_ada, norm_mix, norm_ffn, w_in, conv_w, a_log, dt_bias, norm_o,
              pool_w, pool_scale, w_out, peer_wq, peer_keys, peer_u, peer_v, final_norm):
    bp = x_prompt.shape[0]
    zero_delta = jnp.zeros((bp, DN_HEADS, DN_HEAD_DIM, DN_HEAD_DIM), jnp.float32)
    zero_conv = jnp.zeros((bp, CONV_WIDTH - 1, 3 * DN_WIDTH), x_prompt.dtype)
    zero_pool = jnp.zeros((bp, POOL_BUF, POOL_WIDTH), x_prompt.dtype)
    xp, xs = x_prompt, x_sample
    nd_p, nc_p, npl_p, nd_s, nc_s, npl_s = [], [], [], [], [], []
    for l in range(DEPTH):
        params = (w_ada[l], b_ada[l], norm_mix[l], norm_ffn[l], w_in[l], conv_w[l], a_log[l], dt_bias[l],
                  norm_o[l], pool_w[l], pool_scale[l], w_out[l], peer_wq[l], peer_keys[l], peer_u[l], peer_v[l])
        xp, d, cv, pl = hybrid_layer(xp, c_prompt, zero_delta, zero_conv, zero_pool, 0, *params)
        nd_p.append(d)
        nc_p.append(cv)
        npl_p.append(pl)
        xs, d, cv, pl = hybrid_layer(xs, c_sample, state_delta[l], state_conv[l], state_pool[l], PAST_LEN, *params)
        nd_s.append(d)
        nc_s.append(cv)
        npl_s.append(pl)
    y_prompt = rmsnorm(xp, final_norm)
    y_sample = rmsnorm(xs, final_norm)
    return (y_prompt, y_sample, jnp.stack(nd_p), jnp.stack(nc_p), jnp.stack(npl_p),
            jnp.stack(nd_s), jnp.stack(nc_s), jnp.stack(npl_s))
```

```python
import functools

import numpy as np
import jax
import jax.numpy as jnp
from jax import lax
from jax.experimental import pallas as pl
from jax.experimental.pallas import tpu as pltpu

f32 = jnp.float32
bf16 = jnp.bfloat16

EPS = 1e-6
PEER_TOPK = 16
CONV_WIDTH = 4
POOL_WINDOWS = (2, 4, 8, 16)
POOL_BUF = max(POOL_WINDOWS) - 1
N_MOD = 6
SAMPLE_START = 16384
DELTA_CHUNK = 128
LANES = 128
SUBLANES = 8
VMEM_LIMIT_BYTES = 56 * 1024 * 1024


def _cparams(n_axes):
    return pltpu.CompilerParams(dimension_semantics=("arbitrary",) * n_axes, vmem_limit_bytes=VMEM_LIMIT_BYTES)


def _pick_tile(n, cap, mult):
    t = min(cap, n)
    t -= t % mult
    while t > mult and n % t:
        t -= mult
    assert t >= mult and n % t == 0, (n, cap, mult)
    return t


def _silu(x):
    return x * jax.nn.sigmoid(x)


def _softplus(x):
    return jnp.maximum(x, 0.0) + jnp.log1p(jnp.exp(-jnp.abs(x)))


def _dot(a, b):
    return jnp.dot(a, b, preferred_element_type=f32)


def _dot_nt(a, b):
    return lax.dot_general(a, b, (((1,), (1,)), ((), ())), preferred_element_type=f32)


def _dot_tn(a, b):
    return lax.dot_general(a, b, (((0,), (0,)), ((), ())), preferred_element_type=f32)


def _split_bf16(x):
    hi = x.astype(bf16)
    lo = (x - hi.astype(f32)).astype(bf16)
    return hi, lo


def _dot3(a, b):
    ah, al = _split_bf16(a)
    bh, bl = _split_bf16(b)
    return _dot(ah, bh) + _dot(ah, bl) + _dot(al, bh)


def _ada_call(c_all, w_ada, b_ada):
    depth, d, n6 = w_ada.shape
    r = c_all.shape[0]
    tn = _pick_tile(n6, 512, LANES)

    def kern(c_ref, w_ref, b_ref, o_ref):
        c = c_ref[...]
        s = _silu(c).astype(bf16)
        o_ref[0] = _dot(s, w_ref[0].astype(bf16)) + b_ref[0]

    return pl.pallas_call(
        kern,
        grid=(depth, n6 // tn),
        in_specs=[pl.BlockSpec((r, d), lambda l, j: (0, 0)),
                  pl.BlockSpec((1, d, tn), lambda l, j: (l, 0, j)),
                  pl.BlockSpec((1, 1, tn), lambda l, j: (l, 0, j))],
        out_specs=pl.BlockSpec((1, r, tn), lambda l, j: (l, 0, j)),
        out_shape=jax.ShapeDtypeStruct((depth, r, n6), f32),
        compiler_params=_cparams(2),
        name="ada_mod",
    )(c_all, w_ada, b_ada.reshape(depth, 1, n6))


def _norm_call(x, delta, mod, norm_w, *, gate, shift_scale, n_prompt, seq_len, nb, out_dtype):
    t, d = x.shape
    rb = 2 * nb
    assert n_prompt % rb == 0 and seq_len % rb == 0 and (t - n_prompt) % rb == 0 and nb % SUBLANES == 0
    n_seq = n_prompt // seq_len
    npb = n_prompt // rb
    has_delta = delta is not None
    has_mod = shift_scale is not None
    cols = []
    if has_delta:
        cols.append(gate)
    if has_mod:
        cols.append((shift_scale[0], shift_scale[1]))
        cols.append((shift_scale[0], shift_scale[2]))

    def kern(*refs):
        it = iter(refs)
        x_ref = next(it)
        d_ref = next(it) if has_delta else None
        nw_ref = next(it)
        mrefs = [(next(it), next(it)) for _ in cols]
        xo_ref = next(it) if has_delta else None
        h_ref = next(it)
        i = pl.program_id(0)
        is_s = i >= npb
        seq = jnp.minimum(lax.div(i * rb, seq_len), n_seq - 1)

        def modrow(k):
            s_ref, p_ref = mrefs[k]
            return jnp.where(is_s, s_ref[0], p_ref[0, pl.ds(seq, 1), :])

        for r in range(rb // nb):
            sl = pl.ds(r * nb, nb)
            xv = x_ref[sl, :]
            k = 0
            if has_delta:
                xv = xv + modrow(k) * d_ref[sl, :]
                xo_ref[sl, :] = xv
                k += 1
            y = xv * lax.rsqrt(jnp.mean(xv * xv, axis=-1, keepdims=True) + EPS) * nw_ref[...]
            if has_mod:
                y = y * (1.0 + modrow(k + 1)) + modrow(k)
            h_ref[sl, :] = y.astype(h_ref.dtype)

    row_spec = pl.BlockSpec((rb, d), lambda i: (i, 0))
    in_specs = [row_spec]
    args = [x]
    if has_delta:
        in_specs.append(row_spec)
        args.append(delta)
    in_specs.append(pl.BlockSpec((1, d), lambda i: (0, 0)))
    args.append(norm_w.reshape(1, d))
    for (layer, col) in cols:
        in_specs.append(pl.BlockSpec((1, nb, d), functools.partial(lambda i, l, c: (l, 0, c), l=layer, c=col)))
        in_specs.append(pl.BlockSpec((1, SUBLANES, d),
                                     functools.partial(lambda i, l, c: (l, nb // SUBLANES, c), l=layer, c=col)))
        args += [mod, mod]
    out_specs = []
    out_shape = []
    if has_delta:
        out_specs.append(row_spec)
        out_shape.append(jax.ShapeDtypeStruct((t, d), f32))
    out_specs.append(row_spec)
    out_shape.append(jax.ShapeDtypeStruct((t, d), out_dtype))
    res = pl.pallas_call(kern, grid=(t // rb,), in_specs=in_specs, out_specs=out_specs, out_shape=out_shape,
                         compiler_params=_cparams(1), name="resid_norm_mod")(*args)
    if has_delta:
        return res[0], res[1]
    return x, res[0]


def _matmul_call(a_list, w, layer, n_cols, *, tn, out_dtype=f32, split_lanes=False, name="matmul"):
    m = a_list[0].shape[0]
    k = w.shape[1]
    assert sum(a.shape[1] for a in a_list) == k and n_cols % tn == 0
    tm = _pick_tile(m, 1088, 16)
    na = len(a_list)

    def kern(*refs):
        b_ref = refs[na]
        o_ref = refs[na + 1]
        res = None
        off = 0
        for a_ref in refs[:na]:
            ka = a_ref.shape[1]
            part = _dot(a_ref[...], b_ref[0, off:off + ka, :])
            res = part if res is None else res + part
            off += ka
        if split_lanes:
            for c in range(tn // LANES):
                o_ref[c] = res[:, c * LANES:(c + 1) * LANES].astype(o_ref.dtype)
        else:
            o_ref[...] = res.astype(o_ref.dtype)

    in_specs = [pl.BlockSpec((tm, a.shape[1]), lambda i, j: (i, 0)) for a in a_list]
    in_specs.append(pl.BlockSpec((1, k, tn), functools.partial(lambda i, j, l: (l, 0, j), l=layer)))
    if split_lanes:
        out_spec = pl.BlockSpec((tn // LANES, tm, LANES), lambda i, j: (j, i, 0))
        out_shape = jax.ShapeDtypeStruct((n_cols // LANES, m, LANES), out_dtype)
    else:
        out_spec = pl.BlockSpec((tm, tn), lambda i, j: (i, j))
        out_shape = jax.ShapeDtypeStruct((m, n_cols), out_dtype)
    return pl.pallas_call(kern, grid=(m // tm, n_cols // tn), in_specs=in_specs, out_specs=out_spec,
                          out_shape=out_shape, compiler_params=_cparams(2), name=name)(*a_list, w)


def _pool_group(acc_fn, cur, cnt, w_ref, sc_ref, layer_w, gi, gd):
    lo = gi * gd
    pooled = acc_fn(lo) / cnt - cur[:, lo:lo + gd]
    y = _dot(pooled.astype(bf16), w_ref[0, gi].astype(bf16)) * sc_ref[0, :, lo:lo + gd]
    return y


def _pool_prompt_call(proj, pool_w, pool_scale, layer, *, n_seq, seq_len, t):
    pw = pool_scale.shape[-1]
    ng = len(POOL_WINDOWS)
    gd = pw // ng
    rb = _pick_tile(seq_len, 256, 16)
    nr = seq_len // rb
    hist = 16
    assert hist >= POOL_BUF and gd % LANES == 0

    def kern(p_ref, w_ref, sc_ref, o_ref, xs_ref):
        r = pl.program_id(1)

        @pl.when(r == 0)
        def _():
            xs_ref[0:hist, :] = jnp.zeros((hist, pw), f32)

        cur = p_ref[...]
        xs_ref[hist:hist + rb, :] = cur
        pos = r * rb + lax.broadcasted_iota(jnp.int32, (rb, gd), 0)
        for gi, win in enumerate(POOL_WINDOWS):
            lo = gi * gd
            acc = cur[:, lo:lo + gd]
            for j in range(1, win):
                acc = acc + xs_ref[hist - j:hist - j + rb, lo:lo + gd]
            cnt = jnp.minimum(win, pos + 1).astype(f32)
            pooled = acc / cnt - cur[:, lo:lo + gd]
            y = _dot(pooled.astype(bf16), w_ref[0, gi].astype(bf16)) * sc_ref[0, :, lo:lo + gd]
            o_ref[:, lo:lo + gd] = y.astype(o_ref.dtype)
        xs_ref[0:hist, :] = xs_ref[rb:rb + hist, :]

    depth = pool_w.shape[0]
    return pl.pallas_call(
        kern,
        grid=(n_seq, nr),
        in_specs=[pl.BlockSpec((rb, pw), lambda b, r: (b * nr + r, 0)),
                  pl.BlockSpec((1, ng, gd, gd), lambda b, r: (layer, 0, 0, 0)),
                  pl.BlockSpec((1, 1, pw), lambda b, r: (layer, 0, 0))],
        out_specs=pl.BlockSpec((rb, pw), lambda b, r: (b * nr + r, 0)),
        out_shape=jax.ShapeDtypeStruct((t, pw), bf16),
        scratch_shapes=[pltpu.VMEM((hist + rb, pw), f32)],
        compiler_params=_cparams(2),
        name="pool_prompt",
    )(proj, pool_w, pool_scale.reshape(depth, 1, pw))


def _pool_sample_call(proj, pool_w, pool_scale, buf_t, pool_out, layer, *, n_prompt, nb, ls):
    pw = pool_scale.shape[-1]
    ng = len(POOL_WINDOWS)
    gd = pw // ng
    rows = ls * nb
    assert n_prompt % rows == 0
    rblk = n_prompt // rows
    depth = pool_w.shape[0]

    def kern(p_ref, w_ref, sc_ref, b_ref, prev_ref, o_ref, nb_ref):
        del prev_ref

        def slab(k, lo, hi):
            if k < POOL_BUF:
                return b_ref[0, k, :, lo:hi]
            return p_ref[(k - POOL_BUF) * nb:(k - POOL_BUF + 1) * nb, lo:hi]

        for l in range(ls):
            for gi, win in enumerate(POOL_WINDOWS):
                lo, hi = gi * gd, (gi + 1) * gd
                cur = slab(POOL_BUF + l, lo, hi)
                acc = cur
                for j in range(1, win):
                    acc = acc + slab(POOL_BUF + l - j, lo, hi)
                cnt = float(min(win, SAMPLE_START + l + 1))
                pooled = acc / cnt - cur
                y = _dot(pooled.astype(bf16), w_ref[0, gi].astype(bf16)) * sc_ref[0, :, lo:hi]
                o_ref[l * nb:(l + 1) * nb, lo:hi] = y.astype(o_ref.dtype)
        for j in range(POOL_BUF):
            nb_ref[j] = slab(j + ls, 0, pw)

    return pl.pallas_call(
        kern,
        grid=(1,),
        in_specs=[pl.BlockSpec((rows, pw), lambda i: (rblk, 0)),
                  pl.BlockSpec((1, ng, gd, gd), lambda i: (layer, 0, 0, 0)),
                  pl.BlockSpec((1, 1, pw), lambda i: (layer, 0, 0)),
                  pl.BlockSpec((1, POOL_BUF, nb, pw), lambda i: (layer, 0, 0, 0)),
                  pl.BlockSpec(memory_space=pl.ANY)],
        out_specs=[pl.BlockSpec((rows, pw), lambda i: (rblk, 0)),
                   pl.BlockSpec((POOL_BUF, nb, pw), lambda i: (0, 0, 0))],
        out_shape=[jax.ShapeDtypeStruct(pool_out.shape, pool_out.dtype),
                   jax.ShapeDtypeStruct((POOL_BUF, nb, pw), f32)],
        input_output_aliases={4: 0},
        compiler_params=_cparams(1),
        name="pool_sample",
    )(proj, pool_w, pool_scale.reshape(depth, 1, pw), buf_t, pool_out)


def _tri_level_masks(c):
    i = np.arange(c)[:, None]
    j = np.arange(c)[None, :]
    masks = []
    s = 1
    while s < c:
        masks.append(((i > j) & (i // (2 * s) == j // (2 * s)) & (i // s != j // s)).astype(np.float32))
        s *= 2
    return np.stack(masks)


def _gates(tl, gr_ref):
    beta_all = jax.nn.sigmoid(tl)
    g_all = -jnp.exp(gr_ref[0, 0:1, :]) * _softplus(tl + gr_ref[0, 1:2, :])
    return beta_all, g_all


def _delta_prompt_call(proj, tail, conv_w, gate_rows, norm_o, masks, layer, *, n_seq, seq_len, n_heads, pw, t, hg):
    c = DELTA_CHUNK
    dk = LANES
    dn = n_heads * dk
    gw = hg * dk
    assert seq_len % c == 0 and n_heads % hg == 0 and pw % gw == 0 and dn % gw == 0 and norm_o.shape[-1] == dk
    nc = seq_len // c
    nlev = masks.shape[0]
    qb, kb_, vb, zb = pw // gw, (pw + dn) // gw, (pw + 2 * dn) // gw, (pw + 3 * dn) // gw
    ncb = dn // gw
    hist = SUBLANES
    depth = conv_w.shape[0]

    def kern(q_ref, k_ref, v_ref, z_ref, t_ref, wq_ref, wk_ref, wv_ref, gr_ref, no_ref, m_ref, o_ref, nd_ref,
             s_ref, xs_ref):
        g = pl.program_id(1)
        ci = pl.program_id(2)

        @pl.when(ci == 0)
        def _():
            s_ref[...] = jnp.zeros_like(s_ref)
            xs_ref[:, 0:hist, :] = jnp.zeros((3, hist, gw), f32)

        xs_ref[0, hist:hist + c, :] = q_ref[...]
        xs_ref[1, hist:hist + c, :] = k_ref[...]
        xs_ref[2, hist:hist + c, :] = v_ref[...]

        def conv(i, w_ref):
            base = hist - (CONV_WIDTH - 1)
            acc = xs_ref[i, base:base + c, :] * w_ref[0, 0:1, :]
            for j in range(1, CONV_WIDTH):
                acc = acc + xs_ref[i, base + j:base + j + c, :] * w_ref[0, j:j + 1, :]
            return _silu(acc)

        qx = conv(0, wq_ref)
        kx = conv(1, wk_ref)
        vx = conv(2, wv_ref)
        xs_ref[:, 0:hist, :] = xs_ref[:, c:c + hist, :]

        beta_all, g_all = _gates(t_ref[...], gr_ref)
        lane = lax.broadcasted_iota(jnp.int32, (c, LANES), 1)
        row = lax.broadcasted_iota(jnp.int32, (c, c), 0)
        col = lax.broadcasted_iota(jnp.int32, (c, c), 1)
        tri = row >= col
        eye = (row == col).astype(f32)

        for j in range(hg):
            hd = g * hg + j
            sl = slice(j * dk, (j + 1) * dk)
            bcol = jnp.sum(jnp.where(lane == hd, beta_all, 0.0), axis=-1, keepdims=True)
            gcol = jnp.sum(jnp.where(lane == n_heads + hd, g_all, 0.0), axis=-1, keepdims=True)
            gb = jnp.broadcast_to(gcol, (c, dk))
            sh = 1
            while sh < c:
                gb = gb + jnp.where(row >= sh, pltpu.roll(gb, sh, 0), 0.0)
                sh *= 2
            diff = gb - gb.T
            decay = jnp.where(tri, jnp.exp(jnp.where(tri, diff, 0.0)), 0.0)
            e_g = jnp.exp(gb)
            g_last = gb[c - 1:c, :]

            q = qx[:, sl]
            k = kx[:, sl]
            v = vx[:, sl]
            q = q * lax.rsqrt(jnp.sum(q * q, axis=-1, keepdims=True) + EPS) * (dk ** -0.5)
            k = k * lax.rsqrt(jnp.sum(k * k, axis=-1, keepdims=True) + EPS)
            kbeta = k * bcol
            k16 = k.astype(bf16)
            kkd = _dot_nt(kbeta.astype(bf16), k16) * decay
            tinv = eye - kkd * m_ref[0]
            for lv in range(1, nlev):
                y = _dot3(kkd * m_ref[lv], tinv)
                tinv = tinv - _dot3(tinv, y)
            t16 = tinv.astype(bf16)
            u = _dot(t16, (v * bcol).astype(bf16))
            w = _dot(t16, (kbeta * e_g).astype(bf16))
            attn = _dot_nt(q.astype(bf16), k16) * decay

            s_old = s_ref[j]
            s16 = s_old.astype(bf16)
            v_new = u - _dot(w.astype(bf16), s16)
            vn16 = v_new.astype(bf16)
            o = _dot((q * e_g).astype(bf16), s16) + _dot(attn.astype(bf16), vn16)
            k_dec = k * jnp.exp(g_last - gb)
            s_ref[j] = s_old * jnp.exp(g_last) + _dot_tn(k_dec.astype(bf16), vn16)

            on = o * lax.rsqrt(jnp.mean(o * o, axis=-1, keepdims=True) + EPS) * no_ref[0]
            o_ref[:, sl] = (on * _silu(z_ref[:, sl])).astype(o_ref.dtype)

        @pl.when(ci == nc - 1)
        def _():
            nd_ref[0] = s_ref[...]

    def rows(b, g, ci):
        return b * nc + ci

    return pl.pallas_call(
        kern,
        grid=(n_seq, n_heads // hg, nc),
        in_specs=[pl.BlockSpec((c, gw), lambda b, g, ci: (rows(b, g, ci), qb + g)),
                  pl.BlockSpec((c, gw), lambda b, g, ci: (rows(b, g, ci), kb_ + g)),
                  pl.BlockSpec((c, gw), lambda b, g, ci: (rows(b, g, ci), vb + g)),
                  pl.BlockSpec((c, gw), lambda b, g, ci: (rows(b, g, ci), zb + g)),
                  pl.BlockSpec((c, LANES), lambda b, g, ci: (rows(b, g, ci), 0)),
                  pl.BlockSpec((1, CONV_WIDTH, gw), lambda b, g, ci: (layer, 0, g)),
                  pl.BlockSpec((1, CONV_WIDTH, gw), lambda b, g, ci: (layer, 0, ncb + g)),
                  pl.BlockSpec((1, CONV_WIDTH, gw), lambda b, g, ci: (layer, 0, 2 * ncb + g)),
                  pl.BlockSpec((1, 2, LANES), lambda b, g, ci: (layer, 0, 0)),
                  pl.BlockSpec((1, 1, dk), lambda b, g, ci: (layer, 0, 0)),
                  pl.BlockSpec((nlev, c, c), lambda b, g, ci: (0, 0, 0))],
        out_specs=[pl.BlockSpec((c, gw), lambda b, g, ci: (rows(b, g, ci), g)),
                   pl.BlockSpec((1, hg, dk, dk), lambda b, g, ci: (b, g, 0, 0))],
        out_shape=[jax.ShapeDtypeStruct((t, dn), bf16),
                   jax.ShapeDtypeStruct((n_seq, n_heads, dk, dk), f32)],
        scratch_shapes=[pltpu.VMEM((hg, dk, dk), f32), pltpu.VMEM((3, hist + c, gw), f32)],
        compiler_params=_cparams(3),
        name="delta_prompt",
    )(proj, proj, proj, proj, tail, conv_w, conv_w, conv_w, gate_rows, norm_o.reshape(depth, 1, dk), masks)


def _delta_sample_call(proj, tail, conv_w, gate_rows, norm_o, cbuf_t, state, o_prev, nd_prev, layer, *, n_prompt,
                       nb, ls, n_heads, pw):
    dk = LANES
    dn = n_heads * dk
    rows = ls * nb
    assert n_prompt % rows == 0 and 2 * ls <= SUBLANES
    rblk = n_prompt // rows
    depth = conv_w.shape[0]
    nbuf = CONV_WIDTH - 1
    first = nd_prev is None

    def kern(*refs):
        (q_ref, k_ref, v_ref, z_ref, t_ref, wq_ref, wk_ref, wv_ref, cq_ref, ck_ref, cv_ref, gr_ref, no_ref,
         st_ref) = refs[:14]
        o_ref, nd_ref, lhs_ref, res_ref, kd_ref, vn_ref, egl_ref = refs[-7:]
        h = pl.program_id(0)
        beta_all, g_all = _gates(t_ref[...], gr_ref)
        lane = lax.broadcasted_iota(jnp.int32, (rows, LANES), 1)
        beta = jnp.sum(jnp.where(lane == h, beta_all, 0.0), axis=-1, keepdims=True)
        gg = jnp.sum(jnp.where(lane == n_heads + h, g_all, 0.0), axis=-1, keepdims=True)
        bl = [beta[l * nb:(l + 1) * nb] for l in range(ls)]
        gcum = []
        for l in range(ls):
            gl = gg[l * nb:(l + 1) * nb]
            gcum.append(gl if l == 0 else gcum[-1] + gl)

        def conv(x_ref, c_ref, w_ref):
            xp = [c_ref[0, i] for i in range(nbuf)] + [x_ref[l * nb:(l + 1) * nb, :] for l in range(ls)]
            outs = []
            for l in range(ls):
                acc = xp[l] * w_ref[0, 0:1, :]
                for j in range(1, CONV_WIDTH):
                    acc = acc + xp[l + j] * w_ref[0, j:j + 1, :]
                outs.append(_silu(acc))
            return outs

        qs = conv(q_ref, cq_ref, wq_ref)
        ks = conv(k_ref, ck_ref, wk_ref)
        vs = conv(v_ref, cv_ref, wv_ref)
        qs = [q * lax.rsqrt(jnp.sum(q * q, axis=-1, keepdims=True) + EPS) * (dk ** -0.5) for q in qs]
        ks = [k * lax.rsqrt(jnp.sum(k * k, axis=-1, keepdims=True) + EPS) for k in ks]

        def dec(i, j):
            return jnp.exp(gcum[i] - gcum[j])

        kk = [[bl[i] * jnp.sum(ks[i] * ks[j], axis=-1, keepdims=True) * dec(i, j) for j in range(i)]
              for i in range(ls)]
        attn = [[jnp.sum(qs[i] * ks[j], axis=-1, keepdims=True) * (dec(i, j) if j < i else 1.0)
                 for j in range(i + 1)] for i in range(ls)]
        us, ws = [], []
        for i in range(ls):
            u = bl[i] * vs[i]
            w = bl[i] * ks[i] * jnp.exp(gcum[i])
            for j in range(i):
                u = u - kk[i][j] * us[j]
                w = w - kk[i][j] * ws[j]
            us.append(u)
            ws.append(w)
        for i in range(ls):
            lhs_ref[i] = ws[i]
            lhs_ref[ls + i] = qs[i] * jnp.exp(gcum[i])
        for i in range(2 * ls, SUBLANES):
            lhs_ref[i] = jnp.zeros((nb, dk), f32)

        def state_products(b, carry):
            lhs = lhs_ref[:, pl.ds(b, 1), :].reshape(SUBLANES, dk)
            r = _dot(lhs.astype(bf16), st_ref[0, b, 0].astype(bf16))
            res_ref[:, pl.ds(b, 1), :] = r.reshape(SUBLANES, 1, dk)
            return carry

        lax.fori_loop(0, nb, state_products, 0)

        vns = [us[i] - res_ref[i] for i in range(ls)]
        for i in range(ls):
            o = res_ref[ls + i]
            for j in range(i + 1):
                o = o + attn[i][j] * vns[j]
            on = o * lax.rsqrt(jnp.mean(o * o, axis=-1, keepdims=True) + EPS) * no_ref[0]
            o_ref[i * nb:(i + 1) * nb, :] = (on * _silu(z_ref[i * nb:(i + 1) * nb, :])).astype(o_ref.dtype)
        for i in range(ls):
            kd_ref[i] = ks[i] * jnp.exp(gcum[ls - 1] - gcum[i])
            vn_ref[i] = vns[i]
        for i in range(ls, SUBLANES):
            kd_ref[i] = jnp.zeros((nb, dk), f32)
            vn_ref[i] = jnp.zeros((nb, dk), f32)
        egl_ref[...] = jnp.broadcast_to(jnp.exp(gcum[ls - 1]), (nb, dk))

        def state_update(b, carry):
            kd = kd_ref[:, pl.ds(b, 1), :].reshape(SUBLANES, dk)
            vn = vn_ref[:, pl.ds(b, 1), :].reshape(SUBLANES, dk)
            upd = _dot_tn(kd.astype(bf16), vn.astype(bf16))
            nd_ref[0, b, 0] = st_ref[0, b, 0] * egl_ref[pl.ds(b, 1), :] + upd
            return carry

        lax.fori_loop(0, nb, state_update, 0)

    ncb = n_heads
    in_specs = [pl.BlockSpec((rows, dk), lambda h: (rblk, pw // dk + h)),
                pl.BlockSpec((rows, dk), lambda h: (rblk, (pw + dn) // dk + h)),
                pl.BlockSpec((rows, dk), lambda h: (rblk, (pw + 2 * dn) // dk + h)),
                pl.BlockSpec((rows, dk), lambda h: (rblk, (pw + 3 * dn) // dk + h)),
                pl.BlockSpec((rows, LANES), lambda h: (rblk, 0)),
                pl.BlockSpec((1, CONV_WIDTH, dk), lambda h: (layer, 0, h)),
                pl.BlockSpec((1, CONV_WIDTH, dk), lambda h: (layer, 0, ncb + h)),
                pl.BlockSpec((1, CONV_WIDTH, dk), lambda h: (layer, 0, 2 * ncb + h)),
                pl.BlockSpec((1, nbuf, nb, dk), lambda h: (layer, 0, 0, h)),
                pl.BlockSpec((1, nbuf, nb, dk), lambda h: (layer, 0, 0, ncb + h)),
                pl.BlockSpec((1, nbuf, nb, dk), lambda h: (layer, 0, 0, 2 * ncb + h)),
                pl.BlockSpec((1, 2, LANES), lambda h: (layer, 0, 0)),
                pl.BlockSpec((1, 1, dk), lambda h: (layer, 0, 0)),
                pl.BlockSpec((1, nb, 1, dk, dk), lambda h: (layer, 0, h, 0, 0)),
                pl.BlockSpec(memory_space=pl.ANY)]
    args = [proj, proj, proj, proj, tail, conv_w, conv_w, conv_w, cbuf_t, cbuf_t, cbuf_t, gate_rows,
            norm_o.reshape(depth, 1, dk), state, o_prev]
    aliases = {14: 0}
    if not first:
        in_specs.append(pl.BlockSpec(memory_space=pl.ANY))
        args.append(nd_prev)
        aliases[15] = 1
    return pl.pallas_call(
        kern,
        grid=(n_heads,),
        in_specs=in_specs,
        out_specs=[pl.BlockSpec((rows, dk), lambda h: (rblk, h)),
                   pl.BlockSpec((1, nb, 1, dk, dk), lambda h: (layer, 0, h, 0, 0))],
        out_shape=[jax.ShapeDtypeStruct(o_prev.shape, o_prev.dtype),
                   jax.ShapeDtypeStruct(state.shape, f32)],
        input_output_aliases=aliases,
        scratch_shapes=[pltpu.VMEM((SUBLANES, nb, dk), f32), pltpu.VMEM((SUBLANES, nb, dk), f32),
                        pltpu.VMEM((SUBLANES, nb, dk), f32), pltpu.VMEM((SUBLANES, nb, dk), f32),
                        pltpu.VMEM((nb, dk), f32)],
        compiler_params=_cparams(1),
        name="delta_sample",
    )(*args)


def _topk_call(q3, keys, layer):
    hp2, t, qd = q3.shape
    hp = hp2 // 2
    nk = keys.shape[2]
    kk = PEER_TOPK
    tb = LANES
    assert t % tb == 0
    absent = 99.0

    def kern(q_ref, k_ref, rk2_ref, n1_ref, e1_ref, e2_ref):
        iota_k = lax.broadcasted_iota(jnp.int32, (nk, tb), 0).astype(f32)
        iota_a = lax.broadcasted_iota(jnp.int32, (kk, tb), 0).astype(f32)

        def top_list(s):
            rk = jnp.full((nk, tb), absent, f32)
            vals = jnp.zeros((kk, tb), f32)
            for r in range(kk):
                m = jnp.max(s, axis=0, keepdims=True)
                jsel = jnp.min(jnp.where(s == m, iota_k, float(nk)), axis=0, keepdims=True)
                sel = iota_k == jsel
                rk = jnp.where(sel, float(r), rk)
                s = jnp.where(sel, -jnp.inf, s)
                vals = jnp.where(iota_a == float(r), m, vals)
            return rk, vals

        def merge(v1, v2):
            top = v1[0:1] + v2[0:1]
            n = jnp.zeros((kk, tb), f32)
            f = v1 + v2[0:1]
            z = jnp.zeros((1, tb), f32)
            for _ in range(kk):
                m = jnp.max(f, axis=0, keepdims=True)
                asel = jnp.min(jnp.where(f == m, iota_a, float(kk)), axis=0, keepdims=True)
                sel = iota_a == asel
                z = z + jnp.exp(m - top)
                n = n + jnp.where(sel, 1.0, 0.0)
                nxt = jnp.sum(jnp.where(sel, n, 0.0), axis=0, keepdims=True)
                v1g = jnp.sum(jnp.where(sel, v1, 0.0), axis=0, keepdims=True)
                v2g = jnp.full((1, tb), -jnp.inf, f32)
                for b in range(kk):
                    v2g = jnp.where(nxt == float(b), v2[b:b + 1], v2g)
                f = jnp.where(sel, v1g + v2g, f)
            return n, z

        def head(h, carry):
            s1 = _dot_nt(k_ref[0, 2 * h], q_ref[2 * h].astype(bf16))
            s2 = _dot_nt(k_ref[0, 2 * h + 1], q_ref[2 * h + 1].astype(bf16))
            rk1, v1 = top_list(s1)
            rk2, v2 = top_list(s2)
            n, z = merge(v1, v2)
            n1 = jnp.zeros((nk, tb), f32)
            for a in range(kk):
                n1 = jnp.where(rk1 == float(a), n[a:a + 1], n1)
            rk2_ref[h] = rk2
            n1_ref[h] = n1
            e1_ref[h] = jnp.where(rk1 < float(kk), jnp.exp(s1 - v1[0:1]), 0.0)
            e2_ref[h] = jnp.where(rk2 < float(kk), jnp.exp(s2 - v2[0:1]), 0.0) / z
            return carry

        lax.fori_loop(0, hp, head, 0)

    out_spec = pl.BlockSpec((hp, nk, tb), lambda i: (0, 0, i))
    out_shape = jax.ShapeDtypeStruct((hp, nk, t), f32)
    return pl.pallas_call(
        kern,
        grid=(t // tb,),
        in_specs=[pl.BlockSpec((hp2, tb, qd), lambda i: (0, i, 0)),
                  pl.BlockSpec((1, hp2, nk, qd), lambda i: (layer, 0, 0, 0))],
        out_specs=[out_spec] * 4,
        out_shape=[out_shape] * 4,
        compiler_params=_cparams(1),
        name="peer_topk",
    )(q3, keys)


def _peer_call(h2, u16, v16, rk2, n1, e1, e2, layer):
    t, d = h2.shape
    ne = u16.shape[1]
    hp, nk, _ = rk2.shape
    tb = _pick_tile(t, 512, LANES)
    eb = 2 * nk
    assert ne == nk * nk and ne % eb == 0
    gi = eb // nk

    def kern(h_ref, u_ref, v_ref, rk_ref, n1_ref, e1_ref, e2_ref, o_ref):
        e = pl.program_id(1)

        @pl.when(e == 0)
        def _():
            o_ref[...] = jnp.zeros_like(o_ref)

        at = _dot_nt(u_ref[0], h_ref[...])
        act = 0.5 * at * (1.0 + lax.erf(at * (2.0 ** -0.5)))
        parts = []
        for ii in range(gi):
            i1 = e * gi + ii
            acc = jnp.zeros((nk, tb), f32)
            for hh in range(hp):
                n1row = n1_ref[hh, pl.ds(i1, 1), :]
                e1row = e1_ref[hh, pl.ds(i1, 1), :]
                acc = acc + jnp.where(rk_ref[hh] < n1row, e2_ref[hh] * e1row, 0.0)
            parts.append(acc)
        wt = jnp.concatenate(parts, axis=0) * act
        o_ref[...] += _dot(wt.T.astype(bf16), v_ref[0])

    tk_spec = pl.BlockSpec((hp, nk, tb), lambda i, e: (0, 0, i))
    return pl.pallas_call(
        kern,
        grid=(t // tb, ne // eb),
        in_specs=[pl.BlockSpec((tb, d), lambda i, e: (i, 0)),
                  pl.BlockSpec((1, eb, d), lambda i, e: (layer, e, 0)),
                  pl.BlockSpec((1, eb, d), lambda i, e: (layer, e, 0)),
                  tk_spec, tk_spec, tk_spec, tk_spec],
        out_specs=pl.BlockSpec((tb, d), lambda i, e: (i, 0)),
        out_shape=jax.ShapeDtypeStruct((t, d), f32),
        compiler_params=_cparams(2),
        name="peer_experts",
    )(h2, u16, v16, rk2, n1, e1, e2)


def kernel(x_prompt, x_sample, c_prompt, c_sample, state_delta, state_conv, state_pool, w_ada, b_ada, norm_mix,
           norm_ffn, w_in, conv_w, a_log, dt_bias, norm_o, pool_w, pool_scale, w_out, peer_wq, peer_keys, peer_u,
           peer_v, final_norm):
    bsz, seq_len, d = x_prompt.shape
    nb, ls, _ = x_sample.shape
    depth = w_ada.shape[0]
    pw = pool_scale.shape[1]
    n_heads = a_log.shape[1]
    dk = norm_o.shape[1]
    dn = n_heads * dk
    n_prompt = bsz * seq_len
    t = n_prompt + nb * ls
    main_cols = pw + 4 * dn
    hp = peer_keys.shape[1]
    nk = peer_keys.shape[3]
    assert dk == LANES and nk == LANES and bsz <= SUBLANES and 2 * n_heads <= LANES
    assert seq_len >= POOL_BUF and ls >= CONV_WIDTH - 1 and ls <= POOL_BUF

    x = jnp.concatenate([x_prompt.reshape(n_prompt, d), jnp.transpose(x_sample, (1, 0, 2)).reshape(nb * ls, d)], 0)
    c_all = jnp.concatenate([c_sample, c_prompt, jnp.zeros((SUBLANES - bsz, d), f32)], 0)
    mod = _ada_call(c_all, w_ada, b_ada)

    w_in16 = w_in.astype(bf16)
    w_tail16 = jnp.pad(w_in[:, :, main_cols:], ((0, 0), (0, 0), (0, LANES - 2 * n_heads))).astype(bf16)
    w_out16 = w_out.astype(bf16)
    wq16 = peer_wq.astype(bf16)
    u16 = peer_u.astype(bf16)
    v16 = peer_v.astype(bf16)
    keys16 = peer_keys.reshape(depth, 2 * hp, nk, peer_keys.shape[4]).astype(bf16)
    gate_rows = jnp.zeros((depth, 2, LANES), f32)
    gate_rows = gate_rows.at[:, 0, n_heads:2 * n_heads].set(a_log).at[:, 1, n_heads:2 * n_heads].set(dt_bias)
    masks = jnp.asarray(_tri_level_masks(DELTA_CHUNK))
    cbuf_t = jnp.transpose(state_conv, (0, 2, 1, 3))
    pbuf_t = jnp.transpose(state_pool, (0, 2, 1, 3))

    norm_kw = dict(n_prompt=n_prompt, seq_len=seq_len, nb=nb)
    nd_p, nc_p, np_p, nc_s, np_s = [], [], [], [], []
    nd_s = None
    delta = None
    for l in range(depth):
        x, h = _norm_call(x, delta, mod, norm_mix[l], gate=(l - 1, 5), shift_scale=(l, 0, 1), out_dtype=bf16,
                          **norm_kw)
        proj = _matmul_call([h], w_in16, l, main_cols, tn=_pick_tile(main_cols, 512, LANES), name="mm_in")
        tail = _matmul_call([h], w_tail16, l, LANES, tn=LANES, name="mm_gates")
        proj_p = proj[:n_prompt].reshape(bsz, seq_len, main_cols)
        proj_s = proj[n_prompt:].reshape(ls, nb, main_cols)
        nc_p.append(proj_p[:, seq_len - (CONV_WIDTH - 1):, pw:pw + 3 * dn])
        np_p.append(proj_p[:, seq_len - POOL_BUF:, :pw])
        nc_s.append(jnp.transpose(proj_s[ls - (CONV_WIDTH - 1):, :, pw:pw + 3 * dn], (1, 0, 2)))

        pool_o = _pool_prompt_call(proj, pool_w, pool_scale, l, n_seq=bsz, seq_len=seq_len, t=t)
        pool_o, np_l = _pool_sample_call(proj, pool_w, pool_scale, pbuf_t, pool_o, l, n_prompt=n_prompt, nb=nb,
                                         ls=ls)
        np_s.append(jnp.transpose(np_l, (1, 0, 2)))

        o_g, nd_l = _delta_prompt_call(proj, tail, conv_w, gate_rows, norm_o, masks, l, n_seq=bsz,
                                       seq_len=seq_len, n_heads=n_heads, pw=pw, t=t, hg=4)
        nd_p.append(nd_l)
        o_g, nd_s = _delta_sample_call(proj, tail, conv_w, gate_rows, norm_o, cbuf_t, state_delta, o_g, nd_s, l,
                                       n_prompt=n_prompt, nb=nb, ls=ls, n_heads=n_heads, pw=pw)

        mix = _matmul_call([pool_o, o_g], w_out16, l, d, tn=_pick_tile(d, 512, LANES), name="mm_out")
        x, h2 = _norm_call(x, mix, mod, norm_ffn[l], gate=(l, 2), shift_scale=(l, 3, 4), out_dtype=bf16, **norm_kw)
        nq = wq16.shape[2]
        q3 = _matmul_call([h2], wq16, l, nq, tn=_pick_tile(nq, 512, LANES), split_lanes=True, name="mm_query")
        rk2, n1, e1, e2 = _topk_call(q3, keys16, l)
        delta = _peer_call(h2, u16, v16, rk2, n1, e1, e2, l)

    _, y = _norm_call(x, delta, mod, final_norm, gate=(depth - 1, 5), shift_scale=None, out_dtype=f32, **norm_kw)
    y_prompt = y[:n_prompt].reshape(bsz, seq_len, d)
    y_sample = jnp.transpose(y[n_prompt:].reshape(ls, nb, d), (1, 0, 2))
    return (y_prompt, y_sample, jnp.stack(nd_p), jnp.stack(nc_p), jnp.stack(np_p), nd_s, jnp.stack(nc_s),
            jnp.stack(np_s))
```
